```python
import math
import jax, jax.numpy as jnp
from jax import lax
import numpy as np

D_MODEL = 2048
BATCH = 4
SEQ = 2048
DEPTH = 2
DEC_BATCH = 128
DEC_SEQ = 1
PAST_LEN = 16384
PAGE_SIZE = 128

SSD_HEADDIM = 64
SSD_INNER = D_MODEL
SSD_HEADS = SSD_INNER // SSD_HEADDIM
SSD_GROUPS = 4
SSD_STATE = 128
CONV_WIDTH = 4
XBC_DIM = SSD_INNER + 2 * SSD_GROUPS * SSD_STATE
ML_HEADS = 8
ML_INNER = D_MODEL
ML_DK = ML_INNER // ML_HEADS
ML_DV = ML_INNER // ML_HEADS
GATE_DIM = D_MODEL
IN_SIZES = (SSD_INNER, XBC_DIM, SSD_HEADS, ML_INNER, ML_INNER, ML_INNER, ML_INNER, ML_HEADS, ML_HEADS, GATE_DIM, GATE_DIM)
P_IN = SSD_INNER + XBC_DIM + SSD_HEADS + 4 * ML_INNER + 2 * ML_HEADS + 2 * GATE_DIM
D_FF = 7168
N_EXPERTS = 8
TOP_K = 2
N_DENSE = (DEPTH + 1) // 2
N_MOE = DEPTH // 2
CHUNK = 128
N_MOD = 6
EPS = 1e-6

kernel_name = 'hybrid_ssd_mlstm_adaln_moe_step'


def _offsets(sizes):
    out, acc = [], 0
    for s in sizes[:-1]:
        acc += s
        out.append(acc)
    return out


def rmsnorm(x, g):
    xf = x.astype(jnp.float32)
    y = xf * lax.rsqrt(jnp.mean(xf * xf, axis=-1, keepdims=True) + EPS)
    return (y * g.astype(jnp.float32)).astype(x.dtype)


def group_rmsnorm(x, g, n_groups):
    shp = x.shape
    xf = x.astype(jnp.float32).reshape(shp[:-1] + (n_groups, shp[-1] // n_groups))
    y = xf * lax.rsqrt(jnp.mean(xf * xf, axis=-1, keepdims=True) + EPS)
    return y.reshape(shp) * g.astype(jnp.float32)


def causal_conv(xbc, buf, w, b):
    s = xbc.shape[1]
    xp = jnp.concatenate([buf.astype(xbc.dtype), xbc], axis=1)
    y = b + sum(xp[:, k:k + s] * w[k] for k in range(w.shape[0]))
    return jax.nn.silu(y), xp[:, s:]


def _chunk(t, L):
    bt, s = t.shape[0], t.shape[1]
    return jnp.moveaxis(t.astype(jnp.float32).reshape((bt, s // L, L) + t.shape[2:]), 1, 0)


def _unchunk(t):
    nc, bt, L = t.shape[:3]
    return jnp.moveaxis(t, 0, 1).reshape((bt, nc * L) + t.shape[3:])


def ssd_scan(x, dt, a, bmat, cmat, h0):
    bt, s = x.shape[:2]
    L = math.gcd(s, CHUNK)
    R = SSD_HEADS // SSD_GROUPS
    xs = _chunk(x.reshape(bt, s, SSD_GROUPS, R, SSD_HEADDIM), L)
    dts = _chunk(dt.reshape(bt, s, SSD_GROUPS, R), L)
    bs = _chunk(bmat, L)
    cs = _chunk(cmat, L)
    ag = a.reshape(SSD_GROUPS, R)
    causal = jnp.tril(jnp.ones((L, L), dtype=bool))

    def step(h, inp):
        xc, dtc, bc, cc = inp
        acum = jnp.cumsum(dtc * ag, axis=1)
        at = jnp.moveaxis(acum, 1, -1)
        decay = jnp.exp(jnp.where(causal, at[..., :, None] - at[..., None, :], -jnp.inf))
        cb = jnp.einsum('btgn,bsgn->bgts', cc, bc)
        y_intra = jnp.einsum('bgrts,bsgrp->btgrp', cb[:, :, None] * decay, xc * dtc[..., None])
        y_inter = jnp.einsum('btgn,bgrpn->btgrp', cc, h) * jnp.exp(acum)[..., None]
        a_end = acum[:, -1]
        w_end = jnp.exp(a_end[:, None] - acum) * dtc
        h_new = jnp.exp(a_end)[..., None, None] * h + jnp.einsum('bsgr,bsgrp,bsgn->bgrpn', w_end, xc, bc)
        return h_new, y_intra + y_inter

    h0g = h0.astype(jnp.float32).reshape(bt, SSD_GROUPS, R, SSD_HEADDIM, SSD_STATE)
    h_t, ys = lax.scan(step, h0g, (xs, dts, bs, cs))
    y = _unchunk(ys).reshape(bt, s, SSD_HEADS, SSD_HEADDIM)
    return y, h_t.reshape(bt, SSD_HEADS, SSD_HEADDIM, SSD_STATE)


def mlstm_scan(q, k, v, ig, lf, c0, n0, m0):
    bt, s = q.shape[:2]
    L = math.gcd(s, CHUNK)
    causal = jnp.tril(jnp.ones((L, L), dtype=bool))

    def step(carry, inp):
        c, n, m = carry
        qc, kc, vc, ic, fc = inp
        b = jnp.swapaxes(jnp.cumsum(fc, axis=1), 1, 2)
        ih = jnp.swapaxes(ic, 1, 2)
        dmat = jnp.where(causal, b[..., :, None] - b[..., None, :] + ih[..., None, :], -jnp.inf)
        inter = b + m[..., None]
        m_t = jnp.maximum(inter, jnp.max(dmat, axis=-1))
        w = jnp.exp(dmat - m_t[..., None]) * jnp.einsum('bthd,bshd->bhts', qc, kc)
        g_inter = jnp.exp(inter - m_t)
        num = (jnp.einsum('bhts,bshv->bthv', w, vc)
               + jnp.swapaxes(g_inter, 1, 2)[..., None] * jnp.einsum('bthk,bhkv->bthv', qc, c))
        dot = jnp.sum(w, axis=-1) + g_inter * jnp.einsum('bthk,bhk->bht', qc, n)
        den = jnp.maximum(jnp.abs(dot), jnp.exp(-m_t))
        h = num / jnp.swapaxes(den, 1, 2)[..., None]
        m_last = m_t[..., -1]
        carry_decay = jnp.exp(b[..., -1] + m - m_last)
        w_end = jnp.exp(dmat[..., -1, :] - m_last[..., None])
        c_new = carry_decay[..., None, None] * c + jnp.einsum('bhs,bshk,bshv->bhkv', w_end, kc, vc)
        n_new = carry_decay[..., None] * n + jnp.einsum('bhs,bshk->bhk', w_end, kc)
        return (c_new, n_new, m_last), h

    init = (c0.astype(jnp.float32), n0.astype(jnp.float32), m0.astype(jnp.float32))
    (c_t, n_t, m_t), hs = lax.scan(step, init, (_chunk(q, L), _chunk(k, L), _chunk(v, L), _chunk(ig, L), _chunk(lf, L)))
    return _unchunk(hs), c_t, n_t, m_t


def token_mixers(h, W, l, conv_buf, ssd_h, ml_c, ml_n, ml_m):
    bt, s, _ = h.shape
    f32 = jnp.float32
    proj = h @ W['w_in'][l]
    z, xbc, dt_pre, q, k, v, o_pre, i_pre, f_pre, g_a, g_b = jnp.split(proj, _offsets(IN_SIZES), axis=-1)
    xbc, conv_new = causal_conv(xbc, conv_buf, W['conv_w'][l], W['conv_b'][l])
    xs, bm, cm = jnp.split(xbc, [SSD_INNER, SSD_INNER + SSD_GROUPS * SSD_STATE], axis=-1)
    xs = xs.reshape(bt, s, SSD_HEADS, SSD_HEADDIM)
    bm = bm.reshape(bt, s, SSD_GROUPS, SSD_STATE)
    cm = cm.reshape(bt, s, SSD_GROUPS, SSD_STATE)
    dt = jax.nn.softplus(dt_pre.astype(f32) + W['dt_bias'][l].astype(f32))
    a = -jnp.exp(W['a_log'][l].astype(f32))
    y, ssd_new = ssd_scan(xs, dt, a, bm, cm, ssd_h)
    y = y + xs.astype(f32) * W['d_skip'][l].astype(f32)[:, None]
    y = y.reshape(bt, s, SSD_INNER).astype(h.dtype) * jax.nn.silu(z)
    ssd_out = group_rmsnorm(y, W['ssd_norm_g'][l], SSD_GROUPS).astype(h.dtype)
    q = q.reshape(bt, s, ML_HEADS, ML_DK)
    k = k.reshape(bt, s, ML_HEADS, ML_DK) * (ML_DK ** -0.5)
    v = v.reshape(bt, s, ML_HEADS, ML_DV)
    ig = i_pre.astype(f32) + W['igate_b'][l].astype(f32)
    lf = jax.nn.log_sigmoid(f_pre.astype(f32) + W['fgate_b'][l].astype(f32))
    ht, c_new, n_new, m_new = mlstm_scan(q, k, v, ig, lf, ml_c, ml_n, ml_m)
    ml_out = jax.nn.sigmoid(o_pre) * group_rmsnorm(ht.reshape(bt, s, ML_INNER), W['ml_norm_g'][l], ML_HEADS).astype(h.dtype)
    merged = (jax.nn.sigmoid(g_a) * (ssd_out @ W['w_branch_a'][l])
              + jax.nn.sigmoid(g_b) * (ml_out @ W['w_branch_b'][l]))
    return merged @ W['w_out'][l], (conv_new, ssd_new, c_new, n_new, m_new)


def swiglu(t, wg, wu, wd):
    return (jax.nn.silu(t @ wg) * (t @ wu)) @ wd


def moe_ffn(h, W, j):
    bt, s, d = h.shape
    t = h.reshape(-1, d)
    logits = (t @ W['router_w'][j]).astype(jnp.float32)
    top_v, top_i = lax.top_k(logits, TOP_K)
    gates = jax.nn.softmax(top_v, axis=-1)
    comb = jnp.sum(jax.nn.one_hot(top_i, N_EXPERTS, dtype=jnp.float32) * gates[..., None], axis=1)
    out = jnp.zeros_like(t)
    for e in range(N_EXPERTS):
        ye = swiglu(t, W['moe_w_gate'][j, e], W['moe_w_up'][j, e], W['moe_w_down'][j, e])
        out = out + comb[:, e:e + 1].astype(t.dtype) * ye
    return out.reshape(bt, s, d)


def block(x, c, l, W, conv_buf, ssd_h, ml_c, ml_n, ml_m):
    mod = (jax.nn.silu(c) @ W['ada_w'][l] + W['ada_b'][l])[:, None, :]
    sh1, sc1, g1, sh2, sc2, g2 = jnp.split(mod, N_MOD, axis=-1)
    h = rmsnorm(x, W['norm_mix_g'][l]) * (1 + sc1) + sh1
    mix, new_state = token_mixers(h, W, l, conv_buf, ssd_h, ml_c, ml_n, ml_m)
    x = x + g1 * mix
    h = rmsnorm(x, W['norm_ffn_g'][l]) * (1 + sc2) + sh2
    if l % 2 == 0:
        j = l // 2
        f = swiglu(h, W['ffn_w_gate'][j], W['ffn_w_up'][j], W['ffn_w_down'][j])
    else:
        f = moe_ffn(h, W, l // 2)
    return x + g2 * f, new_state


def setup_inputs(seed: int = 0) -> dict:
    key = jax.random.key(seed)
    ks = iter(jax.random.split(key, 48))
    f32 = jnp.float32
    D = D_MODEL

    def nrm(shape, scale):
        return scale * jax.random.normal(next(ks), shape, f32)

    def unif(shape, lo, hi):
        return jax.random.uniform(next(ks), shape, f32, lo, hi)

    x_prompt = nrm((BATCH, SEQ, D), 1.0)
    x_sample = nrm((DEC_BATCH, DEC_SEQ, D), 1.0)
    state_conv = nrm((DEPTH, DEC_BATCH, CONV_WIDTH - 1, XBC_DIM), 1.0)
    state_ssd = nrm((DEPTH, DEC_BATCH, SSD_HEADS, SSD_HEADDIM, SSD_STATE), 0.5)
    state_mlstm_c = nrm((DEPTH, DEC_BATCH, ML_HEADS, ML_DK, ML_DV), 0.1)
    state_mlstm_n = nrm((DEPTH, DEC_BATCH, ML_HEADS, ML_DK), 0.1)
    state_mlstm_m = unif((DEPTH, DEC_BATCH, ML_HEADS), -2.0, 2.0)
    c_prompt = nrm((BATCH, D), 1.0)
    c_sample = nrm((DEC_BATCH, D), 1.0)
    ada_w = nrm((DEPTH, D, N_MOD * D), 0.5 * D ** -0.5)
    ada_b = nrm((DEPTH, N_MOD * D), 0.02)
    norm_mix_g = 1.0 + nrm((DEPTH, D), 0.05)
    norm_ffn_g = 1.0 + nrm((DEPTH, D), 0.05)
    w_in = nrm((DEPTH, D, P_IN), D ** -0.5)
    conv_w = nrm((DEPTH, CONV_WIDTH, XBC_DIM), CONV_WIDTH ** -0.5)
    conv_b = nrm((DEPTH, XBC_DIM), 0.02)
    dt0 = jnp.exp(unif((DEPTH, SSD_HEADS), math.log(1e-3), math.log(1e-1)))
    dt_bias = dt0 + jnp.log(-jnp.expm1(-dt0))
    a_log = jnp.log(unif((DEPTH, SSD_HEADS), 1.0, 16.0))
    d_skip = 1.0 + nrm((DEPTH, SSD_HEADS), 0.1)
    ssd_norm_g = 1.0 + nrm((DEPTH, SSD_INNER), 0.05)
    igate_b = nrm((DEPTH, ML_HEADS), 0.1)
    fgate_b = unif((DEPTH, ML_HEADS), 3.0, 6.0)
    ml_norm_g = 1.0 + nrm((DEPTH, ML_INNER), 0.05)
    w_branch_a = nrm((DEPTH, SSD_INNER, D), SSD_INNER ** -0.5)
    w_branch_b = nrm((DEPTH, ML_INNER, D), ML_INNER ** -0.5)
    w_out = nrm((DEPTH, D, D), D ** -0.5)
    ffn_w_gate = nrm((N_DENSE, D, D_FF), D ** -0.5)
    ffn_w_up = nrm((N_DENSE, D, D_FF), D ** -0.5)
    ffn_w_down = nrm((N_DENSE, D_FF, D), D_FF ** -0.5)
    router_w = nrm((N_MOE, D, N_EXPERTS), D ** -0.5)
    moe_w_gate = nrm((N_MOE, N_EXPERTS, D, D_FF), D ** -0.5)
    moe_w_up = nrm((N_MOE, N_EXPERTS, D, D_FF), D ** -0.5)
    moe_w_down = nrm((N_MOE, N_EXPERTS, D_FF, D), D_FF ** -0.5)
    final_norm_g = 1.0 + nrm((D,), 0.05)
    return {'x_prompt': x_prompt, 'x_sample': x_sample, 'state_conv': state_conv, 'state_ssd': state_ssd,
            'state_mlstm_c': state_mlstm_c, 'state_mlstm_n': state_mlstm_n, 'state_mlstm_m': state_mlstm_m,
            'c_prompt': c_prompt, 'c_sample': c_sample, 'ada_w': ada_w, 'ada_b': ada_b,
            'norm_mix_g': norm_mix_g, 'norm_ffn_g': norm_ffn_g, 'w_in': w_in, 'conv_w': conv_w, 'conv_b': conv_b,
            'dt_bias': dt_bias, 'a_log': a_log, 'd_skip': d_skip, 'ssd_norm_g': ssd_norm_g,
            'igate_b': igate_b, 'fgate_b': fgate_b, 'ml_norm_g': ml_norm_g,
            'w_branch_a': w_branch_a, 'w_branch_b': w_branch_b, 'w_out': w_out,
            'ffn_w_gate': ffn_w_gate, 'ffn_w_up': ffn_w_up, 'ffn_w_down': ffn_w_down,
            'router_w': router_w, 'moe_w_gate': moe_w_gate, 'moe_w_up': moe_w_up, 'moe_w_down': moe_w_down,
            'final_norm_g': final_norm_g}


def reference(x_prompt, x_sample, state_conv, state_ssd, state_mlstm_c, state_mlstm_n, state_mlstm_m,
              c_prompt, c_sample, ada_w, ada_b, norm_mix_g, norm_ffn_g, w_in, conv_w, conv_b,
              dt_bias, a_log, d_skip, ssd_norm_g, igate_b, fgate_b, ml_norm_g,
              w_branch_a, w_branch_b, w_out, ffn_w_gate, ffn_w_up, ffn_w_down,
              router_w, moe_w_gate, moe_w_up, moe_w_down, final_norm_g):
    W = dict(ada_w=ada_w, ada_b=ada_b, norm_mix_g=norm_mix_g, norm_ffn_g=norm_ffn_g, w_in=w_in,
             conv_w=conv_w, conv_b=conv_b, dt_bias=dt_bias, a_log=a_log, d_skip=d_skip,
             ssd_norm_g=ssd_norm_g, igate_b=igate_b, fgate_b=fgate_b, ml_norm_g=ml_norm_g,
             w_branch_a=w_branch_a, w_branch_b=w_branch_b, w_out=w_out,
             ffn_w_gate=ffn_w_gate, ffn_w_up=ffn_w_up, ffn_w_down=ffn_w_down,
             router_w=router_w, moe_w_gate=moe_w_gate, moe_w_up=moe_w_up, moe_w_down=moe_w_down)
    bp = x_prompt.shape[0]
    f32 = jnp.float32
    conv0 = jnp.zeros((bp, CONV_WIDTH - 1, XBC_DIM), x_prompt.dtype)
    ssd0 = jnp.zeros((bp, SSD_HEADS, SSD_HEADDIM, SSD_STATE), f32)
    c0 = jnp.zeros((bp, ML_HEADS, ML_DK, ML_DV), f32)
    n0 = jnp.zeros((bp, ML_HEADS, ML_DK), f32)
    m0 = jnp.zeros((bp, ML_HEADS), f32)
    xp, xs = x_prompt, x_sample
    new_p, new_s = [], []
    for l in range(DEPTH):
        xp, st_p = block(xp, c_prompt, l, W, conv0, ssd0, c0, n0, m0)
        xs, st_s = block(xs, c_sample, l, W, state_conv[l], state_ssd[l], state_mlstm_c[l],
                         state_mlstm_n[l], state_mlstm_m[l])
        new_p.append(st_p)
        new_s.append(st_s)
    dts = (state_conv.dtype, state_ssd.dtype, state_mlstm_c.dtype, state_mlstm_n.dtype, state_mlstm_m.dtype)
    conv_p, ssd_p, mlc_p, mln_p, mlm_p = [jnp.stack([st[i] for st in new_p]).astype(dts[i]) for i in range(5)]
    conv_s, ssd_s, mlc_s, mln_s, mlm_s = [jnp.stack([st[i] for st in new_s]).astype(dts[i]) for i in range(5)]
    y_prompt = rmsnorm(xp, final_norm_g)
    y_sample = rmsnorm(xs, final_norm_g)
    return (y_prompt, y_sample, conv_p, ssd_p, mlc_p, mln_p, mlm_p, conv_s, ssd_s, mlc_s, mln_s, mlm_s)
```

```python
import functools
import math

import jax
import jax.numpy as jnp
from jax import lax
from jax.experimental import pallas as pl
from jax.experimental.pallas import tpu as pltpu

F32 = jnp.float32
BF16 = jnp.bfloat16

D = 2048
SSD_HEADS = 32
SSD_P = 64
SSD_GROUPS = 4
SSD_N = 128
XBC = D + 2 * SSD_GROUPS * SSD_N
BC = 2 * SSD_GROUPS * SSD_N
ML_HEADS = 8
ML_DK = 256
D_FF = 7168
N_EXPERTS = 8
CHUNK = 128
EPS = 1e-6
N_MOD = 6

COL_Z = 0
COL_X = 2048
COL_BC = 4096
COL_Q = 5120
COL_K = 7168
COL_V = 9216
COL_O = 11264
COL_GA = 13312
COL_GB = 15360
P_MAIN = 17408
LANE_DT = 0
LANE_I = 32
LANE_F = 40

LANES = 128
VMEM_LIMIT = 56 * 1024 * 1024


def _cparams(sem):
    return pltpu.CompilerParams(dimension_semantics=sem, vmem_limit_bytes=VMEM_LIMIT)


def _dot(a, b):
    return jnp.dot(a, b, preferred_element_type=F32)


def _dot_nt(a, b):
    return lax.dot_general(a, b, (((1,), (1,)), ((), ())), preferred_element_type=F32)


def _dot_tn(a, b):
    return lax.dot_general(a, b, (((0,), (0,)), ((), ())), preferred_element_type=F32)


def _split3(a):
    hi = a.astype(BF16)
    r1 = a - hi.astype(F32)
    mid = r1.astype(BF16)
    lo = (r1 - mid.astype(F32)).astype(BF16)
    return hi, mid, lo


def _dot_exact_lhs(sel, a):
    hi, mid, lo = _split3(a)
    return _dot(sel, hi) + _dot(sel, mid) + _dot(sel, lo)


def _dot_exact_rhs(a, sel):
    hi, mid, lo = _split3(a)
    return _dot(hi, sel) + _dot(mid, sel) + _dot(lo, sel)


def _softplus(x):
    return jnp.maximum(x, 0.0) + jnp.log1p(jnp.exp(-jnp.abs(x)))


def _log_sigmoid(x):
    return jnp.minimum(x, 0.0) - jnp.log1p(jnp.exp(-jnp.abs(x)))


def _sigmoid(x):
    return 1.0 / (1.0 + jnp.exp(-x))


def _silu(x):
    return x * _sigmoid(x)


def _tril(n):
    r = lax.broadcasted_iota(jnp.int32, (n, n), 0)
    c = lax.broadcasted_iota(jnp.int32, (n, n), 1)
    return r >= c


def _mm_kernel(x_ref, w_ref, o_ref):
    o_ref[...] = _dot(x_ref[...], w_ref[...]).astype(o_ref.dtype)


def _matmul(x, w, tm, tn, out_dtype):
    m, k = x.shape
    n = w.shape[1]
    return pl.pallas_call(
        _mm_kernel,
        grid=(n // tn, m // tm),
        in_specs=[pl.BlockSpec((tm, k), lambda j, i: (i, 0)),
                  pl.BlockSpec((k, tn), lambda j, i: (0, j))],
        out_specs=pl.BlockSpec((tm, tn), lambda j, i: (i, j)),
        out_shape=jax.ShapeDtypeStruct((m, n), out_dtype),
        compiler_params=_cparams(("arbitrary", "arbitrary")),
    )(x, w)


def _ada_kernel(c_ref, w_ref, b_ref, o_ref):
    c = c_ref[...]
    o_ref[...] = _dot(_silu(c).astype(BF16), w_ref[...]) + b_ref[...]


def _ada_mod(c_all, w, b, tn):
    m, k = c_all.shape
    n = w.shape[1]
    return pl.pallas_call(
        _ada_kernel,
        grid=(n // tn,),
        in_specs=[pl.BlockSpec((m, k), lambda j: (0, 0)),
                  pl.BlockSpec((k, tn), lambda j: (0, j)),
                  pl.BlockSpec((1, tn), lambda j: (0, j))],
        out_specs=pl.BlockSpec((m, tn), lambda j: (0, j)),
        out_shape=jax.ShapeDtypeStruct((m, n), F32),
        compiler_params=_cparams(("arbitrary",)),
    )(c_all, w, b)


def _merge_kernel(xa_ref, xb_ref, ga_ref, gb_ref, wa_ref, wb_ref, o_ref):
    a = _dot(xa_ref[...], wa_ref[...])
    b = _dot(xb_ref[...], wb_ref[...])
    o_ref[...] = (_sigmoid(ga_ref[...]) * a + _sigmoid(gb_ref[...]) * b).astype(o_ref.dtype)


def _merge_mm(xa, xb, proj, wa, wb, tm, tn):
    m, k = xa.shape
    n = wa.shape[1]
    ca, cb = COL_GA // tn, COL_GB // tn
    return pl.pallas_call(
        _merge_kernel,
        grid=(n // tn, m // tm),
        in_specs=[pl.BlockSpec((tm, k), lambda j, i: (i, 0)),
                  pl.BlockSpec((tm, k), lambda j, i: (i, 0)),
                  pl.BlockSpec((tm, tn), lambda j, i: (i, ca + j)),
                  pl.BlockSpec((tm, tn), lambda j, i: (i, cb + j)),
                  pl.BlockSpec((k, tn), lambda j, i: (0, j)),
                  pl.BlockSpec((k, tn), lambda j, i: (0, j))],
        out_specs=pl.BlockSpec((tm, tn), lambda j, i: (i, j)),
        out_shape=jax.ShapeDtypeStruct((m, n), BF16),
        compiler_params=_cparams(("arbitrary", "arbitrary")),
    )(xa, xb, proj, proj, wa, wb)


def _ffn_kernel(te_ref, ta_ref, x_ref, wg_ref, wu_ref, wd_ref, o_ref, acc_ref):
    i = pl.program_id(0)
    j = pl.program_id(1)

    @pl.when(j == 0)
    def _():
        acc_ref[...] = jnp.zeros_like(acc_ref)

    @pl.when(ta_ref[i] != 0)
    def _():
        x = x_ref[...]
        a = _dot(x, wg_ref[0])
        u = _dot(x, wu_ref[0])
        act = (_silu(a) * u).astype(BF16)
        acc_ref[...] += _dot(act, wd_ref[0])

    @pl.when(j == pl.num_programs(1) - 1)
    def _():
        o_ref[...] = acc_ref[...]


def _ffn(x, wg, wu, wd, tile_expert, tile_active, tm, tf):
    r, k = x.shape
    f = wg.shape[2]
    nj = f // tf

    def wcol(i, j, te, ta):
        return (te[i], 0, jnp.where(ta[i] != 0, j, nj - 1))

    def wrow(i, j, te, ta):
        return (te[i], jnp.where(ta[i] != 0, j, nj - 1), 0)

    grid_spec = pltpu.PrefetchScalarGridSpec(
        num_scalar_prefetch=2,
        grid=(r // tm, nj),
        in_specs=[pl.BlockSpec((tm, k), lambda i, j, te, ta: (i, 0)),
                  pl.BlockSpec((1, k, tf), wcol),
                  pl.BlockSpec((1, k, tf), wcol),
                  pl.BlockSpec((1, tf, k), wrow)],
        out_specs=pl.BlockSpec((tm, k), lambda i, j, te, ta: (i, 0)),
        scratch_shapes=[pltpu.VMEM((tm, k), F32)],
    )
    return pl.pallas_call(
        _ffn_kernel,
        grid_spec=grid_spec,
        out_shape=jax.ShapeDtypeStruct((r, k), F32),
        compiler_params=_cparams(("arbitrary", "arbitrary")),
    )(tile_expert, tile_active, x, wg, wu, wd)


def _rms(x, g):
    return x * lax.rsqrt(jnp.mean(x * x, axis=-1, keepdims=True) + EPS) * g


def _pick_mod(is_sample, p_ref, s_ref):
    return jnp.where(is_sample, s_ref[...], p_ref[0:1, :])


def _norm_mod_kernel(x_ref, g_ref, shp_ref, scp_ref, shs_ref, scs_ref, h_ref, *, n_ptiles):
    is_s = pl.program_id(0) >= n_ptiles
    y = _rms(x_ref[...], g_ref[...])
    sc = _pick_mod(is_s, scp_ref, scs_ref)
    sh = _pick_mod(is_s, shp_ref, shs_ref)
    h_ref[...] = (y * (1.0 + sc) + sh).astype(h_ref.dtype)


def _mod_specs(te, tiles_per_seq, n_ptiles, col):
    p_spec = pl.BlockSpec((8, D), lambda t: (jnp.minimum(t, n_ptiles - 1) // tiles_per_seq, col))
    s_spec = pl.BlockSpec((te, D), lambda t: (jnp.maximum(t - n_ptiles, 0), col))
    return p_spec, s_spec


def _norm_mod(x, g, modp, mods, sh_col, sc_col, te, tiles_per_seq, n_ptiles):
    tt = x.shape[0]
    shp, shs = _mod_specs(te, tiles_per_seq, n_ptiles, sh_col)
    scp, scs = _mod_specs(te, tiles_per_seq, n_ptiles, sc_col)
    row = pl.BlockSpec((te, D), lambda t: (t, 0))
    return pl.pallas_call(
        functools.partial(_norm_mod_kernel, n_ptiles=n_ptiles),
        grid=(tt // te,),
        in_specs=[row, pl.BlockSpec((1, D), lambda t: (0, 0)), shp, scp, shs, scs],
        out_specs=row,
        out_shape=jax.ShapeDtypeStruct((tt, D), BF16),
        compiler_params=_cparams(("arbitrary",)),
    )(x, g, modp, modp, mods, mods)


def _resid_norm_kernel(x_ref, d_ref, gp_ref, gs_ref, g_ref, shp_ref, scp_ref, shs_ref, scs_ref,
                       xo_ref, h_ref, *, n_ptiles):
    is_s = pl.program_id(0) >= n_ptiles
    gate = _pick_mod(is_s, gp_ref, gs_ref)
    xn = x_ref[...] + gate * d_ref[...]
    xo_ref[...] = xn
    y = _rms(xn, g_ref[...])
    sc = _pick_mod(is_s, scp_ref, scs_ref)
    sh = _pick_mod(is_s, shp_ref, shs_ref)
    h_ref[...] = (y * (1.0 + sc) + sh).astype(h_ref.dtype)


def _resid_norm(x, delta, g, gmodp, gmods, gate_col, modp, mods, sh_col, sc_col, te, tiles_per_seq, n_ptiles):
    tt = x.shape[0]
    gp, gs = _mod_specs(te, tiles_per_seq, n_ptiles, gate_col)
    shp, shs = _mod_specs(te, tiles_per_seq, n_ptiles, sh_col)
    scp, scs = _mod_specs(te, tiles_per_seq, n_ptiles, sc_col)
    row = pl.BlockSpec((te, D), lambda t: (t, 0))
    return pl.pallas_call(
        functools.partial(_resid_norm_kernel, n_ptiles=n_ptiles),
        grid=(tt // te,),
        in_specs=[row, row, gp, gs, pl.BlockSpec((1, D), lambda t: (0, 0)), shp, scp, shs, scs],
        out_specs=[row, row],
        out_shape=[jax.ShapeDtypeStruct((tt, D), F32), jax.ShapeDtypeStruct((tt, D), BF16)],
        compiler_params=_cparams(("arbitrary",)),
    )(x, delta, gmodp, gmods, g, modp, modp, mods, mods)


def _resid_norm_router_kernel(x_ref, d_ref, gp_ref, gs_ref, g_ref, shp_ref, scp_ref, shs_ref, scs_ref,
                              rw_ref, xo_ref, h_ref, idx_ref, gate_ref, *, n_ptiles):
    is_s = pl.program_id(0) >= n_ptiles
    gate = _pick_mod(is_s, gp_ref, gs_ref)
    xn = x_ref[...] + gate * d_ref[...]
    xo_ref[...] = xn
    y = _rms(xn, g_ref[...])
    sc = _pick_mod(is_s, scp_ref, scs_ref)
    sh = _pick_mod(is_s, shp_ref, shs_ref)
    h = y * (1.0 + sc) + sh
    h_ref[...] = h.astype(h_ref.dtype)
    logits = jnp.dot(h, rw_ref[...], preferred_element_type=F32, precision=lax.Precision.HIGHEST)
    lane = lax.broadcasted_iota(jnp.int32, logits.shape, 1)
    lane_f = lane.astype(F32)
    neg = jnp.float32(-jnp.inf)
    logits = jnp.where(lane < N_EXPERTS, logits, neg)
    v1 = jnp.max(logits, axis=-1, keepdims=True)
    i1 = jnp.min(jnp.where(logits == v1, lane_f, float(LANES)), axis=-1, keepdims=True)
    rest = jnp.where(lane_f == i1, neg, logits)
    v2 = jnp.max(rest, axis=-1, keepdims=True)
    i2 = jnp.min(jnp.where(rest == v2, lane_f, float(LANES)), axis=-1, keepdims=True)
    e2 = jnp.exp(v2 - v1)
    p1 = 1.0 / (1.0 + e2)
    p2 = e2 / (1.0 + e2)
    idx_ref[...] = jnp.where(lane == 0, i1, jnp.where(lane == 1, i2, 0.0)).astype(jnp.int32)
    gate_ref[...] = jnp.where(lane == 0, p1, jnp.where(lane == 1, p2, 0.0))


def _resid_norm_router(x, delta, g, modp, mods, gate_col, sh_col, sc_col, router_w, te, tiles_per_seq, n_ptiles):
    tt = x.shape[0]
    gp, gs = _mod_specs(te, tiles_per_seq, n_ptiles, gate_col)
    shp, shs = _mod_specs(te, tiles_per_seq, n_ptiles, sh_col)
    scp, scs = _mod_specs(te, tiles_per_seq, n_ptiles, sc_col)
    row = pl.BlockSpec((te, D), lambda t: (t, 0))
    small = pl.BlockSpec((te, LANES), lambda t: (t, 0))
    return pl.pallas_call(
        functools.partial(_resid_norm_router_kernel, n_ptiles=n_ptiles),
        grid=(tt // te,),
        in_specs=[row, row, gp, gs, pl.BlockSpec((1, D), lambda t: (0, 0)), shp, scp, shs, scs,
                  pl.BlockSpec((D, LANES), lambda t: (0, 0))],
        out_specs=[row, row, small, small],
        out_shape=[jax.ShapeDtypeStruct((tt, D), F32), jax.ShapeDtypeStruct((tt, D), BF16),
                   jax.ShapeDtypeStruct((tt, LANES), jnp.int32), jax.ShapeDtypeStruct((tt, LANES), F32)],
        compiler_params=_cparams(("arbitrary",)),
    )(x, delta, modp, mods, g, modp, modp, mods, mods, router_w)


def _resid_final_kernel(x_ref, d_ref, gp_ref, gs_ref, g_ref, y_ref, *, n_ptiles):
    is_s = pl.program_id(0) >= n_ptiles
    gate = _pick_mod(is_s, gp_ref, gs_ref)
    xn = x_ref[...] + gate * d_ref[...]
    y_ref[...] = _rms(xn, g_ref[...])


def _resid_final(x, delta, g, modp, mods, gate_col, te, tiles_per_seq, n_ptiles):
    tt = x.shape[0]
    gp, gs = _mod_specs(te, tiles_per_seq, n_ptiles, gate_col)
    row = pl.BlockSpec((te, D), lambda t: (t, 0))
    return pl.pallas_call(
        functools.partial(_resid_final_kernel, n_ptiles=n_ptiles),
        grid=(tt // te,),
        in_specs=[row, row, gp, gs, pl.BlockSpec((1, D), lambda t: (0, 0))],
        out_specs=row,
        out_shape=jax.ShapeDtypeStruct((tt, D), F32),
        compiler_params=_cparams(("arbitrary",)),
    )(x, delta, modp, mods, g)


def _group_norm_store(y, g_ref, o_ref, n_groups):
    w = y.shape[-1] // n_groups
    for gi in range(n_groups):
        yg = y[:, gi * w:(gi + 1) * w]
        o_ref[:, gi * w:(gi + 1) * w] = _rms(yg, g_ref[:, gi * w:(gi + 1) * w]).astype(o_ref.dtype)


def _ssd_prompt_kernel(z_ref, x_ref, bc_ref, dt_ref, cwx_ref, cwbc_ref, cbx_ref, cbbc_ref,
                       dtb_ref, alog_ref, dskip_ref, ng_ref,
                       y_ref, hout_ref, convout_ref,
                       h_scr, bx_scr, bbc_scr, y_scr):
    c = pl.program_id(1)
    nl = CHUNK

    @pl.when(c == 0)
    def _():
        h_scr[...] = jnp.zeros_like(h_scr)
        bx_scr[0:8, :] = jnp.zeros((8, D), F32)
        bbc_scr[0:8, :] = jnp.zeros((8, BC), F32)

    def conv(in_ref, buf, w_ref, b_ref):
        cur = in_ref[...]
        buf[8:8 + nl, :] = cur
        acc = b_ref[...] + w_ref[3:4, :] * cur
        for k in range(3):
            acc = acc + w_ref[k:k + 1, :] * buf[5 + k:5 + k + nl, :]
        tail = buf[nl:nl + 8, :]
        buf[0:8, :] = tail
        return _silu(acc)

    xs = conv(x_ref, bx_scr, cwx_ref, cbx_ref)
    bcm = conv(bc_ref, bbc_scr, cwbc_ref, cbbc_ref)

    @pl.when(c == pl.num_programs(1) - 1)
    def _():
        convout_ref[0, :, 0:D] = bx_scr[0:8, :]
        convout_ref[0, :, D:XBC] = bbc_scr[0:8, :]

    dt = _softplus(dt_ref[...] + dtb_ref[...])
    a = -jnp.exp(alog_ref[...])
    tri = _tril(nl)
    acum = _dot_exact_lhs(tri.astype(BF16), dt * a)
    acum_t = acum.T
    exp_acum = jnp.exp(acum)
    a_end = acum[nl - 1:nl, :]
    w_end = jnp.exp(a_end - acum) * dt
    exp_end = jnp.exp(a_end)

    bcb = bcm.astype(BF16)
    heads_per_group = SSD_HEADS // SSD_GROUPS
    for g in range(SSD_GROUPS):
        bg = bcb[:, g * SSD_N:(g + 1) * SSD_N]
        cg = bcb[:, (SSD_GROUPS + g) * SSD_N:(SSD_GROUPS + g + 1) * SSD_N]
        cb = _dot_nt(cg, bg)
        for pair in range(heads_per_group // 2):
            h0 = g * heads_per_group + 2 * pair
            lo = h0 * SSD_P
            x2 = xs[:, lo:lo + 2 * SSD_P]
            hst = h_scr[lo:lo + 2 * SSD_P, :]
            y_inter = _dot_nt(cg, hst.astype(BF16))
            lane = lax.broadcasted_iota(jnp.int32, (nl, 2 * SSD_P), 1)
            first = lane < SSD_P
            dt2 = jnp.where(first, dt[:, h0:h0 + 1], dt[:, h0 + 1:h0 + 2])
            ea2 = jnp.where(first, exp_acum[:, h0:h0 + 1], exp_acum[:, h0 + 1:h0 + 2])
            we2 = jnp.where(first, w_end[:, h0:h0 + 1], w_end[:, h0 + 1:h0 + 2])
            xdt = (x2 * dt2).astype(BF16)
            y_intra = []
            for q in range(2):
                hh = h0 + q
                diff = acum[:, hh:hh + 1] - acum_t[hh:hh + 1, :]
                decay = jnp.exp(jnp.where(tri, diff, -jnp.inf))
                m = (cb * decay).astype(BF16)
                y_intra.append(_dot(m, xdt))
            y2 = jnp.where(first, y_intra[0], y_intra[1]) + y_inter * ea2
            y_scr[:, lo:lo + 2 * SSD_P] = y2
            xw = (x2 * we2).astype(BF16)
            upd = _dot_tn(xw, bg)
            row = lax.broadcasted_iota(jnp.int32, (2 * SSD_P, SSD_N), 0)
            ee2 = jnp.where(row < SSD_P, exp_end[:, h0:h0 + 1], exp_end[:, h0 + 1:h0 + 2])
            h_scr[lo:lo + 2 * SSD_P, :] = ee2 * hst + upd

    @pl.when(c == pl.num_programs(1) - 1)
    def _():
        hout_ref[0] = h_scr[...]

    y = (y_scr[...] + xs * dskip_ref[...]) * _silu(z_ref[...])
    _group_norm_store(y, ng_ref, y_ref, SSD_GROUPS)


def _ssd_prompt(proj, small, conv_w, conv_b, dtb, alog, dskip_rep, ng, bp, s):
    nc = s // CHUNK
    nl = CHUNK

    def rowblk(width, colblk):
        return pl.BlockSpec((nl, width), lambda b, c: (b * nc + c, colblk))

    def const(shape, blk=(0, 0)):
        return pl.BlockSpec(shape, lambda b, c: blk)

    return pl.pallas_call(
        _ssd_prompt_kernel,
        grid=(bp, nc),
        in_specs=[rowblk(D, COL_Z // D), rowblk(D, COL_X // D), rowblk(BC, COL_BC // BC), rowblk(LANES, 0),
                  const((4, D), (0, 0)), const((4, BC), (0, D // BC)),
                  const((1, D), (0, 0)), const((1, BC), (0, D // BC)),
                  const((1, LANES)), const((1, LANES)), const((1, D)), const((1, D))],
        out_specs=[pl.BlockSpec((nl, D), lambda b, c: (b * nc + c, 0)),
                   pl.BlockSpec((1, SSD_HEADS * SSD_P, SSD_N), lambda b, c: (b, 0, 0)),
                   pl.BlockSpec((1, 8, XBC), lambda b, c: (b, 0, 0))],
        out_shape=[jax.ShapeDtypeStruct((bp * s, D), BF16),
                   jax.ShapeDtypeStruct((bp, SSD_HEADS * SSD_P, SSD_N), F32),
                   jax.ShapeDtypeStruct((bp, 8, XBC), F32)],
        scratch_shapes=[pltpu.VMEM((SSD_HEADS * SSD_P, SSD_N), F32),
                        pltpu.VMEM((nl + 8, D), F32),
                        pltpu.VMEM((nl + 8, BC), F32),
                        pltpu.VMEM((nl, D), F32)],
        compiler_params=_cparams(("arbitrary", "arbitrary")),
    )(proj, proj, proj, small, conv_w, conv_w, conv_b, conv_b, dtb, alog, dskip_rep, ng)


def _mlstm_prompt_kernel(q_ref, k_ref, v_ref, o_ref, g_ref, ib_ref, fb_ref, ng_ref,
                         y_ref, cout_ref, nout_ref, mout_ref,
                         c_scr, n_scr, m_scr):
    h = pl.program_id(1)
    c = pl.program_id(2)
    nl = CHUNK

    @pl.when(c == 0)
    def _():
        c_scr[...] = jnp.zeros_like(c_scr)
        n_scr[...] = jnp.zeros_like(n_scr)
        m_scr[...] = jnp.zeros_like(m_scr)

    gates = g_ref[...]
    lf_all = _log_sigmoid(gates + fb_ref[...])
    ig_all = gates + ib_ref[...]
    tri = _tril(nl)
    b_all = _dot_exact_lhs(tri.astype(BF16), lf_all)
    lane = lax.broadcasted_iota(jnp.int32, (nl, LANES), 1)
    sub = lax.broadcasted_iota(jnp.int32, (LANES, nl), 0)
    bcol = jnp.sum(jnp.where(lane == LANE_F + h, b_all, 0.0), axis=1, keepdims=True)
    igcol = jnp.sum(jnp.where(lane == LANE_I + h, ig_all, 0.0), axis=1, keepdims=True)
    brow = jnp.sum(jnp.where(sub == LANE_F + h, b_all.T, 0.0), axis=0, keepdims=True)
    igrow = jnp.sum(jnp.where(sub == LANE_I + h, ig_all.T, 0.0), axis=0, keepdims=True)

    m_prev = m_scr[0:1, 0:1]
    dmat = jnp.where(tri, bcol - brow + igrow, -jnp.inf)
    inter = bcol + m_prev
    m_t = jnp.maximum(inter, jnp.max(dmat, axis=1, keepdims=True))

    q = q_ref[...]
    ks = k_ref[...] * (ML_DK ** -0.5)
    v = v_ref[...]
    qb = q.astype(BF16)
    vb = v.astype(BF16)
    w = jnp.exp(dmat - m_t) * _dot_nt(qb, ks.astype(BF16))
    g_inter = jnp.exp(inter - m_t)
    cst = c_scr[...]
    nst = n_scr[0:1, :]
    num = _dot(w.astype(BF16), vb) + g_inter * _dot(qb, cst.astype(BF16))
    qn = jnp.sum(q * nst, axis=1, keepdims=True)
    dot = jnp.sum(w, axis=1, keepdims=True) + g_inter * qn
    den = jnp.maximum(jnp.abs(dot), jnp.exp(-m_t))
    ht = num / den

    m_last = m_t[nl - 1:nl, :]
    b_last = bcol[nl - 1:nl, :]
    carry = jnp.exp(b_last + m_prev - m_last)
    w_end = jnp.exp(b_last - bcol + igcol - m_last)
    kw = ks * w_end
    c_new = carry * cst + _dot_tn(kw.astype(BF16), vb)
    n_new = carry * nst + jnp.sum(kw, axis=0, keepdims=True)
    c_scr[...] = c_new
    n_scr[0:1, :] = n_new
    m_scr[...] = jnp.broadcast_to(m_last, m_scr.shape)

    @pl.when(c == pl.num_programs(2) - 1)
    def _():
        cout_ref[0, 0] = c_new
        nout_ref[0, 0] = n_new
        mout_ref[0, 0] = jnp.broadcast_to(m_last, (1, LANES))

    y_ref[...] = (_sigmoid(o_ref[...]) * _rms(ht, ng_ref[...])).astype(y_ref.dtype)


def _mlstm_prompt(proj, small, ib_row, fb_row, ng, bp, s):
    nc = s // CHUNK
    nl = CHUNK

    def headblk(col):
        return pl.BlockSpec((nl, ML_DK), lambda b, h, c: (b * nc + c, col // ML_DK + h))

    return pl.pallas_call(
        _mlstm_prompt_kernel,
        grid=(bp, ML_HEADS, nc),
        in_specs=[headblk(COL_Q), headblk(COL_K), headblk(COL_V), headblk(COL_O),
                  pl.BlockSpec((nl, LANES), lambda b, h, c: (b * nc + c, 0)),
                  pl.BlockSpec((1, LANES), lambda b, h, c: (0, 0)),
                  pl.BlockSpec((1, LANES), lambda b, h, c: (0, 0)),
                  pl.BlockSpec((1, ML_DK), lambda b, h, c: (0, h))],
        out_specs=[pl.BlockSpec((nl, ML_DK), lambda b, h, c: (b * nc + c, h)),
                   pl.BlockSpec((1, 1, ML_DK, ML_DK), lambda b, h, c: (b, h, 0, 0)),
                   pl.BlockSpec((1, 1, 1, ML_DK), lambda b, h, c: (b, h, 0, 0)),
                   pl.BlockSpec((1, 1, 1, LANES), lambda b, h, c: (b, h, 0, 0))],
        out_shape=[jax.ShapeDtypeStruct((bp * s, D), BF16),
                   jax.ShapeDtypeStruct((bp, ML_HEADS, ML_DK, ML_DK), F32),
                   jax.ShapeDtypeStruct((bp, ML_HEADS, 1, ML_DK), F32),
                   jax.ShapeDtypeStruct((bp, ML_HEADS, 1, LANES), F32)],
        scratch_shapes=[pltpu.VMEM((ML_DK, ML_DK), F32),
                        pltpu.VMEM((8, ML_DK), F32),
                        pltpu.VMEM((8, LANES), F32)],
        compiler_params=_cparams(("arbitrary", "arbitrary", "arbitrary")),
    )(proj, proj, proj, proj, small, ib_row, fb_row, ng)


SB = 8


def _stack_blocks(x, n_blocks):
    return jnp.concatenate([x[:, r * LANES:(r + 1) * LANES] for r in range(n_blocks)], axis=0)


def _ssd_sample_kernel(z_ref, x_ref, bc_ref, dt_ref, cs_ref, st_ref, cwx_ref, cwbc_ref, cbx_ref, cbbc_ref,
                       dtb_ref, alog_ref, expand_ref, dskip_ref, ng_ref, *rest):
    y_ref, stout_ref, convout_ref, ycol_scr = rest[-4:]
    st_ref = st_ref.at[0]
    stout_ref = stout_ref.at[0]

    def conv(cur, lo, hi, w_ref, b_ref):
        acc = b_ref[...] + w_ref[3:4, :] * cur
        for k in range(3):
            acc = acc + w_ref[k:k + 1, :] * cs_ref[:, k * XBC + lo:k * XBC + hi]
        return _silu(acc)

    x_new = x_ref[...]
    bc_new = bc_ref[...]
    xs = conv(x_new, 0, D, cwx_ref, cbx_ref)
    bcm = conv(bc_new, D, XBC, cwbc_ref, cbbc_ref)
    convout_ref[:, 0:2 * XBC] = cs_ref[:, XBC:3 * XBC]
    convout_ref[:, 2 * XBC:2 * XBC + D] = x_new
    convout_ref[:, 2 * XBC + D:3 * XBC] = bc_new

    dt = _softplus(dt_ref[...] + dtb_ref[...])
    decay = jnp.exp(dt * (-jnp.exp(alog_ref[...])))
    expand = expand_ref[...]
    dt_rep = _dot_exact_rhs(dt, expand)
    decay_rep = _dot_exact_rhs(decay, expand)
    xdt = xs * dt_rep

    n_blk = D // LANES
    pad = jnp.zeros((LANES - n_blk * SB, LANES), F32) if n_blk * SB < LANES else None

    def columns(a):
        st = _stack_blocks(a, n_blk)
        if pad is not None:
            st = jnp.concatenate([st, pad], axis=0)
        return st.T

    xdt_c = columns(xdt)
    dec_c = columns(decay_rep)

    blocks_per_group = n_blk // SSD_GROUPS
    for j in range(SB):
        for r in range(n_blk):
            g = r // blocks_per_group
            col = r * SB + j
            hs = st_ref[j, r * LANES:(r + 1) * LANES, :]
            brow = bcm[j:j + 1, g * SSD_N:(g + 1) * SSD_N]
            crow = bcm[j:j + 1, (SSD_GROUPS + g) * SSD_N:(SSD_GROUPS + g + 1) * SSD_N]
            h_new = dec_c[:, col:col + 1] * hs + xdt_c[:, col:col + 1] * brow
            stout_ref[j, r * LANES:(r + 1) * LANES, :] = h_new
            ycol_scr[:, col:col + 1] = jnp.sum(h_new * crow, axis=1, keepdims=True)

    yt = ycol_scr[...].T
    y = jnp.concatenate([yt[r * SB:(r + 1) * SB, :] for r in range(n_blk)], axis=1)
    y = (y + xs * dskip_ref[...]) * _silu(z_ref[...])
    _group_norm_store(y, ng_ref, y_ref, SSD_GROUPS)


def _ssd_sample(proj, small, conv_state, ssd_state_all, layer, stacked_out, conv_w, conv_b, dtb, alog, expand,
                dskip_rep, ng, tp, bs):
    r0 = tp // SB
    depth = ssd_state_all.shape[0]

    def rowblk(width, colblk):
        return pl.BlockSpec((SB, width), lambda i: (r0 + i, colblk))

    def const(shape, blk=(0, 0)):
        return pl.BlockSpec(shape, lambda i: blk)

    nfeat = SSD_HEADS * SSD_P
    state_spec = pl.BlockSpec((1, SB, nfeat, SSD_N), lambda i: (layer, i, 0, 0))
    in_specs = [rowblk(D, COL_Z // D), rowblk(D, COL_X // D), rowblk(BC, COL_BC // BC), rowblk(LANES, 0),
                pl.BlockSpec((SB, 3 * XBC), lambda i: (i, 0)),
                state_spec,
                const((4, D), (0, 0)), const((4, BC), (0, D // BC)),
                const((1, D), (0, 0)), const((1, BC), (0, D // BC)),
                const((1, LANES)), const((1, LANES)), const((LANES, D)), const((1, D)), const((1, D))]
    args = [proj, proj, proj, small, conv_state, ssd_state_all, conv_w, conv_w, conv_b, conv_b,
            dtb, alog, expand, dskip_rep, ng]
    aliases = {}
    if stacked_out is not None:
        in_specs.append(pl.BlockSpec(memory_space=pl.ANY))
        args.append(stacked_out)
        aliases = {len(args) - 1: 1}
    return pl.pallas_call(
        _ssd_sample_kernel,
        grid=(bs // SB,),
        in_specs=in_specs,
        out_specs=[pl.BlockSpec((SB, D), lambda i: (i, 0)),
                   state_spec,
                   pl.BlockSpec((SB, 3 * XBC), lambda i: (i, 0))],
        out_shape=[jax.ShapeDtypeStruct((bs, D), F32),
                   jax.ShapeDtypeStruct((depth, bs, nfeat, SSD_N), F32),
                   jax.ShapeDtypeStruct((bs, 3 * XBC), F32)],
        scratch_shapes=[pltpu.VMEM((LANES, LANES), F32)],
        input_output_aliases=aliases,
        compiler_params=_cparams(("arbitrary",)),
    )(*args)


def _mlstm_sample_kernel(q_ref, k_ref, v_ref, o_ref, g_ref, m_ref, c_ref, n_ref, ib_ref, fb_ref, ng_ref, *rest):
    y_ref, cout_ref, nout_ref, mout_ref = rest[-4:]
    c_ref = c_ref.at[0]
    cout_ref = cout_ref.at[0]
    h = pl.program_id(1)
    gates = g_ref[...]
    lane = lax.broadcasted_iota(jnp.int32, (SB, LANES), 1)

    def pick(a, l):
        return jnp.sum(jnp.where(lane == l, a, 0.0), axis=1, keepdims=True)

    lf = pick(_log_sigmoid(gates + fb_ref[...]), LANE_F + h)
    ig = pick(gates + ib_ref[...], LANE_I + h)
    m_in = m_ref[...]
    lane_h = lax.broadcasted_iota(jnp.int32, m_in.shape, 1)
    m_prev = jnp.sum(jnp.where(lane_h == h, m_in, 0.0), axis=1, keepdims=True)
    inter = lf + m_prev
    m_t = jnp.maximum(inter, ig)
    g_inter = jnp.exp(inter - m_t)
    w_end = jnp.exp(ig - m_t)
    mout_ref[0] = jnp.broadcast_to(m_t, (SB, LANES))

    q = q_ref[...]
    ks = k_ref[...] * (ML_DK ** -0.5)
    v = v_ref[...]
    kw = ks * w_end
    n_halves = ML_DK // LANES
    parts = [q[:, r * LANES:(r + 1) * LANES] for r in range(n_halves)]
    parts += [kw[:, r * LANES:(r + 1) * LANES] for r in range(n_halves)]
    parts.append(jnp.zeros((LANES - 2 * n_halves * SB, LANES), F32))
    cols = jnp.concatenate(parts, axis=0).T

    hts = []
    for j in range(SB):
        gj = g_inter[j:j + 1, :]
        vrow = v[j:j + 1, :]
        num = jnp.zeros((1, ML_DK), F32)
        for r in range(n_halves):
            cblk = c_ref[j, 0, r * LANES:(r + 1) * LANES, :]
            qc = cols[:, r * SB + j:r * SB + j + 1]
            kc = cols[:, (n_halves + r) * SB + j:(n_halves + r) * SB + j + 1]
            c_new = gj * cblk + kc * vrow
            cout_ref[j, 0, r * LANES:(r + 1) * LANES, :] = c_new
            num = num + jnp.sum(qc * c_new, axis=0, keepdims=True)
        n_new = gj * n_ref[j, 0] + kw[j:j + 1, :]
        nout_ref[j, 0] = n_new
        qn = jnp.sum(q[j:j + 1, :] * n_new, axis=1, keepdims=True)
        den = jnp.maximum(jnp.abs(qn), jnp.exp(-m_t[j:j + 1, :]))
        hts.append(num / den)
    ht = jnp.concatenate(hts, axis=0)
    y_ref[...] = _sigmoid(o_ref[...]) * _rms(ht, ng_ref[...])


def _mlstm_sample(proj, small, m_state, c_state_all, layer, stacked_out, n_state, ib_row, fb_row, ng, tp, bs):
    r0 = tp // SB
    depth = c_state_all.shape[0]

    def headblk(col):
        return pl.BlockSpec((SB, ML_DK), lambda i, h: (r0 + i, col // ML_DK + h))

    c_spec = pl.BlockSpec((1, SB, 1, ML_DK, ML_DK), lambda i, h: (layer, i, h, 0, 0))
    in_specs = [headblk(COL_Q), headblk(COL_K), headblk(COL_V), headblk(COL_O),
                pl.BlockSpec((SB, LANES), lambda i, h: (r0 + i, 0)),
                pl.BlockSpec((SB, ML_HEADS), lambda i, h: (i, 0)),
                c_spec,
                pl.BlockSpec((SB, 1, 1, ML_DK), lambda i, h: (i, h, 0, 0)),
                pl.BlockSpec((1, LANES), lambda i, h: (0, 0)),
                pl.BlockSpec((1, LANES), lambda i, h: (0, 0)),
                pl.BlockSpec((1, ML_DK), lambda i, h: (0, h))]
    args = [proj, proj, proj, proj, small, m_state, c_state_all, n_state, ib_row, fb_row, ng]
    aliases = {}
    if stacked_out is not None:
        in_specs.append(pl.BlockSpec(memory_space=pl.ANY))
        args.append(stacked_out)
        aliases = {len(args) - 1: 1}
    return pl.pallas_call(
        _mlstm_sample_kernel,
        grid=(bs // SB, ML_HEADS),
        in_specs=in_specs,
        out_specs=[pl.BlockSpec((SB, ML_DK), lambda i, h: (i, h)),
                   c_spec,
                   pl.BlockSpec((SB, 1, 1, ML_DK), lambda i, h: (i, h, 0, 0)),
                   pl.BlockSpec((1, SB, LANES), lambda i, h: (h, i, 0))],
        out_shape=[jax.ShapeDtypeStruct((bs, D), F32),
                   jax.ShapeDtypeStruct((depth, bs, ML_HEADS, ML_DK, ML_DK), F32),
                   jax.ShapeDtypeStruct((bs, ML_HEADS, 1, ML_DK), F32),
                   jax.ShapeDtypeStruct((ML_HEADS, bs, LANES), F32)],
        input_output_aliases=aliases,
        compiler_params=_cparams(("arbitrary", "arbitrary")),
    )(*args)


def _largest_tile(n, cap, mult):
    best = mult
    t = mult
    while t <= min(n, cap):
        if n % t == 0:
            best = t
        t += mult
    return best


def _pad_lanes(vec, offset):
    out = jnp.zeros((1, LANES), F32)
    return lax.dynamic_update_slice(out, vec.astype(F32)[None, :], (0, offset))


def _moe_dispatch(top_i, tm, n_tiles):
    tt = top_i.shape[0]
    eid = top_i.reshape(-1)
    order = jnp.argsort(eid, stable=True)
    counts = jnp.bincount(eid, length=N_EXPERTS)
    padded = ((counts + tm - 1) // tm) * tm
    gstart = jnp.cumsum(padded) - padded
    cstart = jnp.cumsum(counts) - counts
    sorted_e = eid[order]
    pos_sorted = gstart[sorted_e] + (jnp.arange(2 * tt) - cstart[sorted_e])
    row_token = jnp.zeros((n_tiles * tm,), jnp.int32).at[pos_sorted].set((order // 2).astype(jnp.int32))
    slot_pos = jnp.zeros((2 * tt,), jnp.int32).at[order].set(pos_sorted.astype(jnp.int32)).reshape(tt, 2)
    tile_start = jnp.arange(n_tiles) * tm
    gend = gstart + padded
    tile_e = jnp.sum((tile_start[:, None] >= gend[None, :]).astype(jnp.int32), axis=1)
    active = (tile_e < N_EXPERTS).astype(jnp.int32)
    last_e = jnp.max(jnp.where(counts > 0, jnp.arange(N_EXPERTS), 0))
    tile_e = jnp.where(active != 0, tile_e, last_e).astype(jnp.int32)
    return row_token, slot_pos, tile_e, active


def kernel(x_prompt, x_sample, state_conv, state_ssd, state_mlstm_c, state_mlstm_n, state_mlstm_m,
           c_prompt, c_sample, ada_w, ada_b, norm_mix_g, norm_ffn_g, w_in, conv_w, conv_b,
           dt_bias, a_log, d_skip, ssd_norm_g, igate_b, fgate_b, ml_norm_g,
           w_branch_a, w_branch_b, w_out, ffn_w_gate, ffn_w_up, ffn_w_down,
           router_w, moe_w_gate, moe_w_up, moe_w_down, final_norm_g):
    bp, s, _ = x_prompt.shape
    bs = x_sample.shape[0]
    depth = w_in.shape[0]
    tp = bp * s
    tt = tp + bs
    assert s % CHUNK == 0 and bs % SB == 0 and tp % SB == 0
    te = math.gcd(CHUNK, bs)
    assert s % te == 0 and te % 8 == 0
    tiles_per_seq = s // te
    n_ptiles = tp // te
    tm = _largest_tile(tt, 1024, 128)
    tn = 1024

    x = jnp.concatenate([x_prompt.reshape(tp, D), x_sample.reshape(bs, D)], axis=0)

    n_c = bp + bs
    n_c_pad = -(-n_c // 8) * 8
    c_all = jnp.zeros((n_c_pad, D), F32).at[:bp].set(c_prompt).at[bp:n_c].set(c_sample)

    expand = (jnp.arange(D)[None, :] // SSD_P == jnp.arange(LANES)[:, None]).astype(BF16)

    ssd_state_all = state_ssd.reshape(depth, bs, SSD_HEADS * SSD_P, SSD_N)
    conv_state_all = state_conv.reshape(depth, bs, 3 * XBC)
    n_state_all = state_mlstm_n.reshape(depth, bs, ML_HEADS, 1, ML_DK)

    new_p, new_s = [], []
    delta = None
    ssd_s = None
    mlc_s = None
    for l in range(depth):
        mod = _ada_mod(c_all, ada_w[l].astype(BF16), ada_b[l][None, :], tn)
        modp = jnp.repeat(mod[:bp], 8, axis=0)
        mods = mod[bp:n_c]
        if l == 0:
            h1 = _norm_mod(x, norm_mix_g[l][None, :], modp, mods, 0, 1, te, tiles_per_seq, n_ptiles)
        else:
            x, h1 = _resid_norm(x, delta, norm_mix_g[l][None, :], prev_modp, prev_mods, 5, modp, mods, 0, 1,
                                te, tiles_per_seq, n_ptiles)

        wl = w_in[l]
        w_main = jnp.concatenate([wl[:, :5120], wl[:, 5152:13344], wl[:, 13360:]], axis=1).astype(BF16)
        w_small = jnp.concatenate([wl[:, 5120:5152], wl[:, 13344:13360],
                                   jnp.zeros((D, LANES - 48), F32)], axis=1).astype(BF16)
        proj = _matmul(h1, w_main, tm, tn, F32)
        small = _matmul(h1, w_small, tm, LANES, F32)

        dtb = _pad_lanes(dt_bias[l], LANE_DT)
        alog = _pad_lanes(a_log[l], LANE_DT)
        dskip_rep = jnp.repeat(d_skip[l].astype(F32), SSD_P)[None, :]
        ssd_g = ssd_norm_g[l][None, :]
        ib_row = _pad_lanes(igate_b[l], LANE_I)
        fb_row = _pad_lanes(fgate_b[l], LANE_F)
        ml_g = ml_norm_g[l][None, :]

        ya_p, ssd_p, conv_p = _ssd_prompt(proj, small, conv_w[l], conv_b[l][None, :], dtb, alog,
                                          dskip_rep, ssd_g, bp, s)
        yb_p, mlc_p, mln_p, mlm_p = _mlstm_prompt(proj, small, ib_row, fb_row, ml_g, bp, s)
        ya_s, ssd_s, conv_s = _ssd_sample(proj, small, conv_state_all[l], ssd_state_all, l, ssd_s,
                                          conv_w[l], conv_b[l][None, :], dtb, alog, expand, dskip_rep, ssd_g, tp, bs)
        yb_s, mlc_s, mln_s, mlm_s = _mlstm_sample(proj, small, state_mlstm_m[l], state_mlstm_c, l, mlc_s,
                                                  n_state_all[l], ib_row, fb_row, ml_g, tp, bs)
        new_p.append((conv_p[:, 5:8, :], ssd_p.reshape(bp, SSD_HEADS, SSD_P, SSD_N), mlc_p,
                      mln_p.reshape(bp, ML_HEADS, ML_DK), mlm_p[:, :, 0, 0]))
        new_s.append((conv_s.reshape(bs, 3, XBC), mln_s.reshape(bs, ML_HEADS, ML_DK),
                      jnp.transpose(mlm_s[:, :, 0])))

        ya = jnp.concatenate([ya_p, ya_s.astype(BF16)], axis=0)
        yb = jnp.concatenate([yb_p, yb_s.astype(BF16)], axis=0)
        merged = _merge_mm(ya, yb, proj, w_branch_a[l].astype(BF16), w_branch_b[l].astype(BF16), tm, tn)
        mix = _matmul(merged, w_out[l].astype(BF16), tm, tn, F32)

        if l % 2 == 0:
            j = l // 2
            x, h2 = _resid_norm(x, mix, norm_ffn_g[l][None, :], modp, mods, 2, modp, mods, 3, 4,
                                te, tiles_per_seq, n_ptiles)
            n_t = tt // tm
            delta = _ffn(h2, ffn_w_gate[j].astype(BF16)[None], ffn_w_up[j].astype(BF16)[None],
                         ffn_w_down[j].astype(BF16)[None],
                         jnp.zeros((n_t,), jnp.int32), jnp.ones((n_t,), jnp.int32), tm, 512)
        else:
            j = l // 2
            rw = jnp.zeros((D, LANES), F32).at[:, :N_EXPERTS].set(router_w[j])
            x, h2, top_i, top_g = _resid_norm_router(x, mix, norm_ffn_g[l][None, :], modp, mods, 2, 3, 4, rw,
                                                     te, tiles_per_seq, n_ptiles)
            tme = 512 if tt >= 4096 else 128
            n_tiles = (2 * tt) // tme + N_EXPERTS
            row_token, slot_pos, tile_e, active = _moe_dispatch(top_i[:, :2], tme, n_tiles)
            xs_sorted = jnp.take(h2, row_token, axis=0)
            ys = _ffn(xs_sorted, moe_w_gate[j].astype(BF16), moe_w_up[j].astype(BF16),
                      moe_w_down[j].astype(BF16), tile_e, active, tme, 512)
            delta = (top_g[:, 0:1] * jnp.take(ys, slot_pos[:, 0], axis=0)
                     + top_g[:, 1:2] * jnp.take(ys, slot_pos[:, 1], axis=0))
        prev_modp, prev_mods = modp, mods

    y = _resid_final(x, delta, final_norm_g[None, :], prev_modp, prev_mods, 5, te, tiles_per_seq, n_ptiles)
    y_prompt = y[:tp].reshape(bp, s, D)
    y_sample = y[tp:].reshape(bs, 1, D)
    conv_p, ssd_p, mlc_p, mln_p, mlm_p = [jnp.stack([st[i] for st in new_p]) for i in range(5)]
    conv_s, mln_s, mlm_s = [jnp.stack([st[i] for st in new_s]) for i in range(3)]
    ssd_s = ssd_s.reshape(depth, bs, SSD_HEADS, SSD_P, SSD_N)
    return (y_prompt, y_sample, conv_p, ssd_p, mlc_p, mln_p, mlm_p, conv_s, ssd_s, mlc_s, mln_s, mlm_s)
```
